```python
import jax, jax.numpy as jnp
from jax import lax
import numpy as np

D_MODEL = 1024
BATCH = 8
SEQ = 4096
DEPTH = 4

CHUNK = 64
RET_HEADS = 8
RET_QK_DIM = 64
RET_V_DIM = 128
RET_QK_WIDTH = RET_HEADS * RET_QK_DIM
RET_V_WIDTH = RET_HEADS * RET_V_DIM
CONV_WIDTH = D_MODEL
CONV_KERNEL = 31
ROPE_BASE = 10000.0
NORM_EPS = 1e-6
IN_SIZES = (
    RET_QK_WIDTH,
    RET_QK_WIDTH,
    RET_V_WIDTH,
    RET_V_WIDTH,
    CONV_WIDTH,
    CONV_WIDTH,
    CONV_WIDTH,
    D_MODEL,
    D_MODEL,
)
IN_WIDTH = sum(IN_SIZES)

kernel_name = "hybrid_retention_conformer_conv_block"


def _split_points():
    pts, acc = [], 0
    for s in IN_SIZES[:-1]:
        acc += s
        pts.append(acc)
    return pts


def rms_norm(x, g):
    xf = x.astype(jnp.float32)
    y = xf * lax.rsqrt(jnp.mean(xf * xf, axis=-1, keepdims=True) + NORM_EPS)
    return (y * g.astype(jnp.float32)).astype(x.dtype)


def layer_norm(x, g, b):
    xf = x.astype(jnp.float32)
    mu = jnp.mean(xf, axis=-1, keepdims=True)
    var = jnp.mean(jnp.square(xf - mu), axis=-1, keepdims=True)
    y = (xf - mu) * lax.rsqrt(var + NORM_EPS)
    return (y * g.astype(jnp.float32) + b.astype(jnp.float32)).astype(x.dtype)


def rotary(t, pos):
    half = t.shape[-1] // 2
    inv = ROPE_BASE ** (-jnp.arange(half, dtype=jnp.float32) / half)
    ang = pos[:, None] * inv[None, :]
    cos = jnp.cos(ang)[None, :, None, :]
    sin = jnp.sin(ang)[None, :, None, :]
    t1, t2 = t[..., :half], t[..., half:]
    return jnp.concatenate([t1 * cos - t2 * sin, t1 * sin + t2 * cos], axis=-1)


def chunk_retention(q, k, v):
    B, S, H, dk = q.shape
    dv = v.shape[-1]
    n = S // CHUNK
    qc = q.reshape(B, n, CHUNK, H, dk)
    kc = k.reshape(B, n, CHUNK, H, dk)
    vc = v.reshape(B, n, CHUNK, H, dv)
    log_g = jnp.log1p(-jnp.exp2(-5.0 - jnp.arange(H, dtype=jnp.float32)))
    idx = jnp.arange(CHUNK, dtype=jnp.float32)
    intra_decay = jnp.exp(log_g[:, None, None] * jnp.abs(idx[:, None] - idx[None, :]))
    scores = jnp.einsum('bnahd,bnjhd->bnhaj', qc, kc) * intra_decay
    intra = jnp.einsum('bnhaj,bnjhe->bnahe', scores, vc)
    k_decay = jnp.exp(log_g[None, :] * (CHUNK - idx[:, None]))
    kv = jnp.einsum('bnjhd,jh,bnjhe->nbhde', kc, k_decay, vc)
    chunk_decay = jnp.exp(log_g * CHUNK)[None, :, None, None]

    def step(state, kv_i):
        return state * chunk_decay + kv_i, state

    _, states = lax.scan(step, jnp.zeros((B, H, dk, dv), jnp.float32), kv)
    q_decay = jnp.exp(log_g[None, :] * idx[:, None])
    cross = jnp.einsum('bnahd,ah,nbhde->bnahe', qc, q_decay, states)
    return (intra + cross).reshape(B, S, H, dv)


def causal_depthwise_conv(u, w, b):
    K = w.shape[0]
    up = jnp.pad(u, ((0, 0), (K - 1, 0), (0, 0)))
    y = lax.conv_general_dilated(up, w[:, None, :].astype(u.dtype), window_strides=(1,), padding='VALID',
                                 dimension_numbers=('NWC', 'WIO', 'NWC'),
                                 feature_group_count=u.shape[-1])
    return y + b.astype(u.dtype)


def hybrid_layer(x, pre_g, w_in, w_ret_out, conv_w, conv_b, ln_g, ln_b, w_conv_out, w_o, post_g):
    B, S, _ = x.shape
    h = rms_norm(x, pre_g)
    proj = h @ w_in
    q, k, v, g_ret, glu_a, glu_b, g_conv, m_ret, m_conv = jnp.split(proj, _split_points(), axis=-1)

    pos = jnp.arange(S, dtype=jnp.float32)
    qh = rotary(q.astype(jnp.float32).reshape(B, S, RET_HEADS, RET_QK_DIM), pos) * (RET_QK_DIM ** -0.5)
    kh = rotary(k.astype(jnp.float32).reshape(B, S, RET_HEADS, RET_QK_DIM), pos)
    vh = v.astype(jnp.float32).reshape(B, S, RET_HEADS, RET_V_DIM)
    r = chunk_retention(qh, kh, vh)
    mu = jnp.mean(r, axis=-1, keepdims=True)
    var = jnp.mean(jnp.square(r - mu), axis=-1, keepdims=True)
    r = ((r - mu) * lax.rsqrt(var + NORM_EPS)).reshape(B, S, RET_V_WIDTH).astype(x.dtype)
    ret_out = (r * jax.nn.silu(g_ret)) @ w_ret_out

    u = glu_a * jax.nn.sigmoid(glu_b)
    c = causal_depthwise_conv(u, conv_w, conv_b)
    c = jax.nn.silu(layer_norm(c, ln_g, ln_b))
    conv_out = (c * jax.nn.silu(g_conv)) @ w_conv_out

    y = jax.nn.sigmoid(m_ret) * ret_out + jax.nn.sigmoid(m_conv) * conv_out
    y = y @ w_o
    return x + rms_norm(y, post_g)


def setup_inputs(seed: int = 0) -> dict:
    key = jax.random.key(seed)
    ks = jax.random.split(key, 12)
    f = jnp.float32
    x = jax.random.normal(ks[0], (BATCH, SEQ, D_MODEL), f)
    pre_norm_g = 1.0 + 0.05 * jax.random.normal(ks[1], (DEPTH, D_MODEL), f)
    w_in = jax.random.normal(ks[2], (DEPTH, D_MODEL, IN_WIDTH), f) * D_MODEL ** -0.5
    w_ret_out = jax.random.normal(ks[3], (DEPTH, RET_V_WIDTH, D_MODEL), f) * RET_V_WIDTH ** -0.5
    conv_w = jax.random.normal(ks[4], (DEPTH, CONV_KERNEL, CONV_WIDTH), f) * CONV_KERNEL ** -0.5
    conv_b = 0.02 * jax.random.normal(ks[5], (DEPTH, CONV_WIDTH), f)
    conv_ln_g = 1.0 + 0.05 * jax.random.normal(ks[6], (DEPTH, CONV_WIDTH), f)
    conv_ln_b = 0.02 * jax.random.normal(ks[7], (DEPTH, CONV_WIDTH), f)
    w_conv_out = jax.random.normal(ks[8], (DEPTH, CONV_WIDTH, D_MODEL), f) * CONV_WIDTH ** -0.5
    w_o = jax.random.normal(ks[9], (DEPTH, D_MODEL, D_MODEL), f) * D_MODEL ** -0.5
    post_norm_g = 1.0 + 0.05 * jax.random.normal(ks[10], (DEPTH, D_MODEL), f)
    return {"x": x, "pre_norm_g": pre_norm_g, "w_in": w_in, "w_ret_out": w_ret_out,
            "conv_w": conv_w, "conv_b": conv_b, "conv_ln_g": conv_ln_g, "conv_ln_b": conv_ln_b,
            "w_conv_out": w_conv_out, "w_o": w_o, "post_norm_g": post_norm_g}


def reference(x, pre_norm_g, w_in, w_ret_out, conv_w, conv_b, conv_ln_g, conv_ln_b, w_conv_out, w_o, post_norm_g):
    h = x
    for l in range(DEPTH):
        h = hybrid_layer(h, pre_norm_g[l], w_in[l], w_ret_out[l], conv_w[l], conv_b[l],
                         conv_ln_g[l], conv_ln_b[l], w_conv_out[l], w_o[l], post_norm_g[l])
    return h
```

```python
import functools

import jax
import jax.numpy as jnp
from jax import lax
from jax.experimental import pallas as pl
from jax.experimental.pallas import tpu as pltpu

D_MODEL = 1024
CHUNK = 64
HEADS = 8
QK_DIM = 64
V_DIM = 128
PAIRS = HEADS // 2
QK_WIDTH = HEADS * QK_DIM
CONV_K = 31
ROPE_BASE = 10000.0
EPS = 1e-6

SUBLANES = 8
LANES = 128
HALO = 32
ROWS = 256
CONV_ROWS = 32
VMEM_LIMIT_BYTES = 60 * 1024 * 1024

_Q, _K, _V, _GRET, _GLUA, _GLUB, _GCONV, _MRET, _MCONV, _END = (
    0, 512, 1024, 2048, 3072, 4096, 5120, 6144, 7168, 8192)

BF16 = jnp.bfloat16
F32 = jnp.float32


def _sigmoid(z):
    return 1.0 / (1.0 + jnp.exp(-z))


def _silu(z):
    return z * _sigmoid(z)


def _layer_kernel(x_ref, cos_ref, sin_ref, pre_g_ref, w_in_ref, w_ro_ref, cw_ref, cb_ref,
                  lng_ref, lnb_ref, w_co_ref, w_o_ref, post_g_ref,
                  idec_ref, qdec_ref, kdec_ref, cdec_ref, bmask_ref,
                  o_ref, state_ref, ubuf_ref, r_ref, c_ref):
    rows = x_ref.shape[0]

    @pl.when(pl.program_id(1) == 0)
    def _():
        state_ref[...] = jnp.zeros_like(state_ref)
        ubuf_ref[0:HALO, :] = jnp.zeros((HALO, D_MODEL), F32)

    x = x_ref[...]
    ms = jnp.mean(x * x, axis=-1, keepdims=True)
    h = (x * lax.rsqrt(ms + EPS) * pre_g_ref[...]).astype(BF16)

    def proj(lo, hi):
        return jnp.dot(h, w_in_ref[:, lo:hi], preferred_element_type=F32)

    cos = jnp.concatenate([cos_ref[...]] * PAIRS, axis=1)
    sin = jnp.concatenate([sin_ref[...]] * PAIRS, axis=1)
    lane = lax.broadcasted_iota(jnp.int32, (rows, QK_WIDTH), 1)
    first_half = (lane % QK_DIM) < (QK_DIM // 2)

    def rope(t):
        partner = jnp.where(first_half,
                            pltpu.roll(t, QK_WIDTH - QK_DIM // 2, 1),
                            pltpu.roll(t, QK_DIM // 2, 1))
        return t * cos + partner * sin

    q = rope(proj(_Q, _K)) * (QK_DIM ** -0.5)
    k = rope(proj(_K, _V))
    v = proj(_V, _GRET)

    lane_qk = lax.broadcasted_iota(jnp.int32, (CHUNK, 2 * QK_DIM), 1)
    lane_v = lax.broadcasted_iota(jnp.int32, (CHUNK, 2 * V_DIM), 1)
    qdec = qdec_ref[...]
    kdec = kdec_ref[...]
    bmask = bmask_ref[...]
    for c in range(rows // CHUNK):
        rs = slice(c * CHUNK, (c + 1) * CHUNK)
        for p in range(PAIRS):
            qs = slice(p * 2 * QK_DIM, (p + 1) * 2 * QK_DIM)
            vs = slice(p * 2 * V_DIM, (p + 1) * 2 * V_DIM)
            qp, kp, vp = q[rs, qs], k[rs, qs], v[rs, vs]
            k_rows = jnp.concatenate([jnp.where(lane_qk < QK_DIM, kp, 0.0),
                                      jnp.where(lane_qk >= QK_DIM, kp, 0.0)], axis=0)
            s = lax.dot_general(qp.astype(BF16), k_rows.astype(BF16),
                                (((1,), (1,)), ((), ())), preferred_element_type=F32)
            pmat = s * idec_ref[p]
            kv = lax.dot_general((kp * kdec[:, qs]).astype(BF16), vp.astype(BF16),
                                 (((0,), (0,)), ((), ())), preferred_element_type=F32)
            st = state_ref[p]
            v_diag = jnp.concatenate([jnp.where(lane_v < V_DIM, vp, 0.0),
                                      jnp.where(lane_v >= V_DIM, vp, 0.0)], axis=0)
            lhs = jnp.concatenate([pmat, qp * qdec[:, qs]], axis=1).astype(BF16)
            rhs = jnp.concatenate([v_diag, st], axis=0).astype(BF16)
            r_ref[rs, vs] = jnp.dot(lhs, rhs, preferred_element_type=F32)
            state_ref[p] = st * cdec_ref[p] + kv * bmask

    g_ret = proj(_GRET, _GLUA)
    gated = []
    for hd in range(HEADS):
        rh = r_ref[:, hd * V_DIM:(hd + 1) * V_DIM]
        mu = jnp.mean(rh, axis=-1, keepdims=True)
        dlt = rh - mu
        var = jnp.mean(dlt * dlt, axis=-1, keepdims=True)
        gated.append(dlt * lax.rsqrt(var + EPS))
    a = (jnp.concatenate(gated, axis=1) * _silu(g_ret)).astype(BF16)
    ret_out = jnp.dot(a, w_ro_ref[...], preferred_element_type=F32)

    u = proj(_GLUA, _GLUB) * _sigmoid(proj(_GLUB, _GCONV))
    ubuf_ref[HALO:HALO + rows, :] = u
    groups = CONV_ROWS // SUBLANES
    for g in range(rows // CONV_ROWS):
        base = g * CONV_ROWS + HALO - (CONV_K - 1)
        acc = jnp.zeros((groups, SUBLANES, D_MODEL), F32)
        for t in range(CONV_K):
            win = ubuf_ref[base + t:base + t + CONV_ROWS, :]
            acc = acc + win.reshape(groups, SUBLANES, D_MODEL) * cw_ref[t][None]
        c_ref[g * CONV_ROWS:(g + 1) * CONV_ROWS, :] = acc.reshape(CONV_ROWS, D_MODEL)
    ubuf_ref[0:HALO, :] = ubuf_ref[rows:rows + HALO, :]

    cv = c_ref[...] + cb_ref[...]
    mu = jnp.mean(cv, axis=-1, keepdims=True)
    dlt = cv - mu
    var = jnp.mean(dlt * dlt, axis=-1, keepdims=True)
    cn = _silu(dlt * lax.rsqrt(var + EPS) * lng_ref[...] + lnb_ref[...])
    cg = (cn * _silu(proj(_GCONV, _MRET))).astype(BF16)
    conv_out = jnp.dot(cg, w_co_ref[...], preferred_element_type=F32)

    y = (_sigmoid(proj(_MRET, _MCONV)) * ret_out
         + _sigmoid(proj(_MCONV, _END)) * conv_out).astype(BF16)
    y = jnp.dot(y, w_o_ref[...], preferred_element_type=F32)
    ms = jnp.mean(y * y, axis=-1, keepdims=True)
    o_ref[...] = x + y * lax.rsqrt(ms + EPS) * post_g_ref[...]


def _const_spec(shape):
    return pl.BlockSpec(shape, lambda b, s: (0,) * len(shape), pipeline_mode=pl.Buffered(1))


def _layer_spec(layer, shape):
    return pl.BlockSpec((None,) + shape, lambda b, s: (layer,) + (0,) * len(shape),
                        pipeline_mode=pl.Buffered(1))


def _layer_call(layer, x, tables, params):
    batch, seq, _ = x.shape
    cos, sin, idec, qdec, kdec, cdec, bmask = tables
    pre_g, w_in, w_ro, cw, cb, lng, lnb, w_co, w_o, post_g = params
    row_vec = (1, D_MODEL)
    in_specs = [
        pl.BlockSpec((None, ROWS, D_MODEL), lambda b, s: (b, s, 0)),
        pl.BlockSpec((ROWS, 2 * QK_DIM), lambda b, s: (s, 0)),
        pl.BlockSpec((ROWS, 2 * QK_DIM), lambda b, s: (s, 0)),
        _layer_spec(layer, row_vec),
        _layer_spec(layer, (D_MODEL, _END)),
        _layer_spec(layer, (D_MODEL, D_MODEL)),
        _layer_spec(layer, (CONV_K, SUBLANES, D_MODEL)),
        _layer_spec(layer, row_vec),
        _layer_spec(layer, row_vec),
        _layer_spec(layer, row_vec),
        _layer_spec(layer, (D_MODEL, D_MODEL)),
        _layer_spec(layer, (D_MODEL, D_MODEL)),
        _layer_spec(layer, row_vec),
        _const_spec(idec.shape),
        _const_spec(qdec.shape),
        _const_spec(kdec.shape),
        _const_spec(cdec.shape),
        _const_spec(bmask.shape),
    ]
    return pl.pallas_call(
        _layer_kernel,
        grid=(batch, seq // ROWS),
        in_specs=in_specs,
        out_specs=pl.BlockSpec((None, ROWS, D_MODEL), lambda b, s: (b, s, 0)),
        out_shape=jax.ShapeDtypeStruct(x.shape, x.dtype),
        scratch_shapes=[
            pltpu.VMEM((PAIRS, 2 * QK_DIM, 2 * V_DIM), F32),
            pltpu.VMEM((HALO + ROWS, D_MODEL), F32),
            pltpu.VMEM((ROWS, D_MODEL), F32),
            pltpu.VMEM((ROWS, D_MODEL), F32),
        ],
        compiler_params=pltpu.CompilerParams(
            dimension_semantics=("arbitrary", "arbitrary"),
            vmem_limit_bytes=VMEM_LIMIT_BYTES),
        name=f"hybrid_layer_{layer}",
    )(x, cos, sin, pre_g, w_in, w_ro, cw, cb, lng, lnb, w_co, w_o, post_g,
      idec, qdec, kdec, cdec, bmask)


def _tables(seq):
    half = QK_DIM // 2
    inv = ROPE_BASE ** (-jnp.arange(half, dtype=F32) / half)
    ang = jnp.arange(seq, dtype=F32)[:, None] * inv[None, :]
    cos = jnp.tile(jnp.cos(ang), (1, 4))
    sin = jnp.tile(jnp.concatenate([-jnp.sin(ang), jnp.sin(ang)], axis=1), (1, 2))

    log_g = jnp.log1p(-jnp.exp2(-5.0 - jnp.arange(HEADS, dtype=F32)))
    idx = jnp.arange(CHUNK, dtype=F32)
    intra = jnp.exp(log_g[:, None, None] * jnp.abs(idx[:, None] - idx[None, :]))
    idec = intra.reshape(PAIRS, 2, CHUNK, CHUNK).transpose(0, 2, 1, 3).reshape(PAIRS, CHUNK, 2 * CHUNK)
    kdec = jnp.repeat(jnp.exp(log_g[None, :] * (CHUNK - idx[:, None])), QK_DIM, axis=1)
    qdec = jnp.repeat(jnp.exp(log_g[None, :] * idx[:, None]), QK_DIM, axis=1)
    chunk_decay = jnp.exp(log_g * CHUNK).reshape(PAIRS, 2)
    cdec = jnp.broadcast_to(jnp.repeat(chunk_decay, QK_DIM, axis=1)[:, :, None],
                            (PAIRS, 2 * QK_DIM, 2 * V_DIM))
    row_head = jnp.arange(2 * QK_DIM)[:, None] // QK_DIM
    col_head = jnp.arange(2 * V_DIM)[None, :] // V_DIM
    bmask = (row_head == col_head).astype(F32)
    return cos, sin, idec, qdec, kdec, cdec, bmask


def kernel(x, pre_norm_g, w_in, w_ret_out, conv_w, conv_b, conv_ln_g, conv_ln_b, w_conv_out, w_o, post_norm_g):
    depth = w_in.shape[0]
    batch, seq, d_model = x.shape
    assert d_model == D_MODEL and seq % ROWS == 0 and ROWS % CHUNK == 0 and ROWS % CONV_ROWS == 0
    assert w_in.shape[1:] == (D_MODEL, _END) and conv_w.shape[1:] == (CONV_K, D_MODEL)
    tables = _tables(seq)
    row = lambda p: p.reshape(depth, 1, D_MODEL)
    params = (row(pre_norm_g), w_in.astype(BF16), w_ret_out.astype(BF16),
              jnp.broadcast_to(conv_w[:, :, None, :], (depth, CONV_K, SUBLANES, D_MODEL)),
              row(conv_b), row(conv_ln_g), row(conv_ln_b),
              w_conv_out.astype(BF16), w_o.astype(BF16), row(post_norm_g))
    h = x
    for layer in range(depth):
        h = _layer_call(layer, h, tables, params)
    return h
```

```python
import jax
import jax.numpy as jnp
from jax import lax
from jax.experimental import pallas as pl
from jax.experimental.pallas import tpu as pltpu

D_MODEL = 1024
CHUNK = 64
HEADS = 8
QK_DIM = 64
V_DIM = 128
PAIRS = HEADS // 2
QK_WIDTH = HEADS * QK_DIM
CONV_K = 31
ROPE_BASE = 10000.0
EPS = 1e-6

SUBLANES = 8
LANES = 128
LANE_BLOCKS = D_MODEL // LANES
HALO = 32
ROWS = 256
CONV_ROWS = 64
WGROUP = 512
VMEM_LIMIT_BYTES = 60 * 1024 * 1024

_Q, _K, _V, _GRET, _GLUA, _GLUB, _GCONV, _MRET, _MCONV, _END = (
    0, 1, 2, 4, 6, 8, 10, 12, 14, 16)
IN_WIDTH = _END * WGROUP

BF16 = jnp.bfloat16
F32 = jnp.float32


def _sigmoid(z):
    return 1.0 / (1.0 + jnp.exp(-z))


def _silu(z):
    return z * _sigmoid(z)


def _grouped_dot(lhs, w_ref, lo, hi):
    parts = [jnp.dot(lhs, w_ref[g], preferred_element_type=F32) for g in range(lo, hi)]
    return parts[0] if len(parts) == 1 else jnp.concatenate(parts, axis=1)


def _layer_kernel(x_ref, cos_ref, sin_ref, pre_g_ref, w_in_ref, w_ro_ref, cw_ref, cb_ref,
                  lng_ref, lnb_ref, w_co_ref, w_o_ref, post_g_ref,
                  idec_ref, qdec_ref, kdec_ref, cdec_ref, bmask_ref,
                  o_ref, state_ref, ubuf_ref, r_ref):
    rows = x_ref.shape[0]

    @pl.when(pl.program_id(1) == 0)
    def _():
        state_ref[...] = jnp.zeros_like(state_ref)
        ubuf_ref[:, 0:HALO, :] = jnp.zeros((LANE_BLOCKS, HALO, LANES), F32)

    x = x_ref[...]
    ms = jnp.mean(x * x, axis=-1, keepdims=True)
    h = (x * lax.rsqrt(ms + EPS) * pre_g_ref[...]).astype(BF16)

    def proj(lo, hi):
        return _grouped_dot(h, w_in_ref, lo, hi)

    u = proj(_GLUA, _GLUB) * _sigmoid(proj(_GLUB, _GCONV))
    for blk in range(LANE_BLOCKS):
        ubuf_ref[blk, HALO:HALO + rows, :] = u[:, blk * LANES:(blk + 1) * LANES]
    groups = CONV_ROWS // SUBLANES
    conv_blocks = []
    for blk in range(LANE_BLOCKS):
        taps = [cw_ref[t, :, blk * LANES:(blk + 1) * LANES] for t in range(CONV_K)]
        pieces = []
        for g in range(rows // CONV_ROWS):
            base = g * CONV_ROWS + HALO - (CONV_K - 1)
            acc = None
            for t in range(CONV_K):
                win = ubuf_ref[blk, base + t:base + t + CONV_ROWS, :]
                term = win.reshape(groups, SUBLANES, LANES) * taps[t][None]
                acc = term if acc is None else acc + term
            pieces.append(acc.reshape(CONV_ROWS, LANES))
        conv_blocks.append(jnp.concatenate(pieces, axis=0))
        ubuf_ref[blk, 0:HALO, :] = ubuf_ref[blk, rows:rows + HALO, :]
    cv = jnp.concatenate(conv_blocks, axis=1) + cb_ref[...]

    cos = jnp.concatenate([cos_ref[...]] * PAIRS, axis=1)
    sin = jnp.concatenate([sin_ref[...]] * PAIRS, axis=1)
    lane = lax.broadcasted_iota(jnp.int32, (rows, QK_WIDTH), 1)
    first_half = (lane % QK_DIM) < (QK_DIM // 2)

    def rope(t):
        partner = jnp.where(first_half,
                            pltpu.roll(t, QK_WIDTH - QK_DIM // 2, 1),
                            pltpu.roll(t, QK_DIM // 2, 1))
        return t * cos + partner * sin

    q = rope(proj(_Q, _K)) * (QK_DIM ** -0.5)
    k = rope(proj(_K, _V))
    v = proj(_V, _GRET)

    lane_qk = lax.broadcasted_iota(jnp.int32, (CHUNK, 2 * QK_DIM), 1)
    lane_v = lax.broadcasted_iota(jnp.int32, (CHUNK, 2 * V_DIM), 1)
    qdec = qdec_ref[...]
    kdec = kdec_ref[...]
    bmask = bmask_ref[...]
    for c in range(rows // CHUNK):
        rs = slice(c * CHUNK, (c + 1) * CHUNK)
        for p in range(PAIRS):
            qs = slice(p * 2 * QK_DIM, (p + 1) * 2 * QK_DIM)
            vs = slice(p * 2 * V_DIM, (p + 1) * 2 * V_DIM)
            qp, kp, vp = q[rs, qs], k[rs, qs], v[rs, vs]
            k_rows = jnp.concatenate([jnp.where(lane_qk < QK_DIM, kp, 0.0),
                                      jnp.where(lane_qk >= QK_DIM, kp, 0.0)], axis=0)
            s = lax.dot_general(qp.astype(BF16), k_rows.astype(BF16),
                                (((1,), (1,)), ((), ())), preferred_element_type=F32)
            pmat = s * idec_ref[p]
            kv = lax.dot_general((kp * kdec[:, qs]).astype(BF16), vp.astype(BF16),
                                 (((0,), (0,)), ((), ())), preferred_element_type=F32)
            st = state_ref[p]
            v_diag = jnp.concatenate([jnp.where(lane_v < V_DIM, vp, 0.0),
                                      jnp.where(lane_v >= V_DIM, vp, 0.0)], axis=0)
            lhs = jnp.concatenate([pmat, qp * qdec[:, qs]], axis=1).astype(BF16)
            rhs = jnp.concatenate([v_diag, st], axis=0).astype(BF16)
            r_ref[rs, vs] = jnp.dot(lhs, rhs, preferred_element_type=F32)
            state_ref[p] = st * cdec_ref[p] + kv * bmask

    g_ret = proj(_GRET, _GLUA)
    gated = []
    for hd in range(HEADS):
        rh = r_ref[:, hd * V_DIM:(hd + 1) * V_DIM]
        mu = jnp.mean(rh, axis=-1, keepdims=True)
        dlt = rh - mu
        var = jnp.mean(dlt * dlt, axis=-1, keepdims=True)
        gated.append(dlt * lax.rsqrt(var + EPS))
    a = (jnp.concatenate(gated, axis=1) * _silu(g_ret)).astype(BF16)
    ret_out = _grouped_dot(a, w_ro_ref, 0, D_MODEL // WGROUP)

    mu = jnp.mean(cv, axis=-1, keepdims=True)
    dlt = cv - mu
    var = jnp.mean(dlt * dlt, axis=-1, keepdims=True)
    cn = _silu(dlt * lax.rsqrt(var + EPS) * lng_ref[...] + lnb_ref[...])
    cg = (cn * _silu(proj(_GCONV, _MRET))).astype(BF16)
    conv_out = _grouped_dot(cg, w_co_ref, 0, D_MODEL // WGROUP)

    y = (_sigmoid(proj(_MRET, _MCONV)) * ret_out
         + _sigmoid(proj(_MCONV, _END)) * conv_out).astype(BF16)
    y = _grouped_dot(y, w_o_ref, 0, D_MODEL // WGROUP)
    ms = jnp.mean(y * y, axis=-1, keepdims=True)
    o_ref[...] = x + y * lax.rsqrt(ms + EPS) * post_g_ref[...]


def _const_spec(shape):
    return pl.BlockSpec(shape, lambda b, s: (0,) * len(shape), pipeline_mode=pl.Buffered(1))


def _layer_spec(layer, shape):
    return pl.BlockSpec((None,) + shape, lambda b, s: (layer,) + (0,) * len(shape),
                        pipeline_mode=pl.Buffered(1))


def _layer_call(layer, x, tables, params):
    batch, seq, _ = x.shape
    cos, sin, idec, qdec, kdec, cdec, bmask = tables
    pre_g, w_in, w_ro, cw, cb, lng, lnb, w_co, w_o, post_g = params
    row_vec = (1, D_MODEL)
    square = (D_MODEL // WGROUP, D_MODEL, WGROUP)
    in_specs = [
        pl.BlockSpec((None, ROWS, D_MODEL), lambda b, s: (b, s, 0)),
        pl.BlockSpec((ROWS, 2 * QK_DIM), lambda b, s: (s, 0)),
        pl.BlockSpec((ROWS, 2 * QK_DIM), lambda b, s: (s, 0)),
        _layer_spec(layer, row_vec),
        _layer_spec(layer, (_END, D_MODEL, WGROUP)),
        _layer_spec(layer, square),
        _layer_spec(layer, (CONV_K, SUBLANES, D_MODEL)),
        _layer_spec(layer, row_vec),
        _layer_spec(layer, row_vec),
        _layer_spec(layer, row_vec),
        _layer_spec(layer, square),
        _layer_spec(layer, square),
        _layer_spec(layer, row_vec),
        _const_spec(idec.shape),
        _const_spec(qdec.shape),
        _const_spec(kdec.shape),
        _const_spec(cdec.shape),
        _const_spec(bmask.shape),
    ]
    return pl.pallas_call(
        _layer_kernel,
        grid=(batch, seq // ROWS),
        in_specs=in_specs,
        out_specs=pl.BlockSpec((None, ROWS, D_MODEL), lambda b, s: (b, s, 0)),
        out_shape=jax.ShapeDtypeStruct(x.shape, x.dtype),
        scratch_shapes=[
            pltpu.VMEM((PAIRS, 2 * QK_DIM, 2 * V_DIM), F32),
            pltpu.VMEM((LANE_BLOCKS, HALO + ROWS, LANES), F32),
            pltpu.VMEM((ROWS, D_MODEL), F32),
        ],
        compiler_params=pltpu.CompilerParams(
            dimension_semantics=("arbitrary", "arbitrary"),
            vmem_limit_bytes=VMEM_LIMIT_BYTES),
        name=f"hybrid_layer_{layer}",
    )(x, cos, sin, pre_g, w_in, w_ro, cw, cb, lng, lnb, w_co, w_o, post_g,
      idec, qdec, kdec, cdec, bmask)


def _tables(seq):
    half = QK_DIM // 2
    inv = ROPE_BASE ** (-jnp.arange(half, dtype=F32) / half)
    ang = jnp.arange(seq, dtype=F32)[:, None] * inv[None, :]
    cos = jnp.tile(jnp.cos(ang), (1, 4))
    sin = jnp.tile(jnp.concatenate([-jnp.sin(ang), jnp.sin(ang)], axis=1), (1, 2))

    log_g = jnp.log1p(-jnp.exp2(-5.0 - jnp.arange(HEADS, dtype=F32)))
    idx = jnp.arange(CHUNK, dtype=F32)
    intra = jnp.exp(log_g[:, None, None] * jnp.abs(idx[:, None] - idx[None, :]))
    idec = intra.reshape(PAIRS, 2, CHUNK, CHUNK).transpose(0, 2, 1, 3).reshape(PAIRS, CHUNK, 2 * CHUNK)
    kdec = jnp.repeat(jnp.exp(log_g[None, :] * (CHUNK - idx[:, None])), QK_DIM, axis=1)
    qdec = jnp.repeat(jnp.exp(log_g[None, :] * idx[:, None]), QK_DIM, axis=1)
    chunk_decay = jnp.exp(log_g * CHUNK).reshape(PAIRS, 2)
    cdec = jnp.broadcast_to(jnp.repeat(chunk_decay, QK_DIM, axis=1)[:, :, None],
                            (PAIRS, 2 * QK_DIM, 2 * V_DIM))
    row_head = jnp.arange(2 * QK_DIM)[:, None] // QK_DIM
    col_head = jnp.arange(2 * V_DIM)[None, :] // V_DIM
    bmask = (row_head == col_head).astype(F32)
    return cos, sin, idec, qdec, kdec, cdec, bmask


def _column_groups(w):
    depth, k, n = w.shape
    return w.astype(BF16).reshape(depth, k, n // WGROUP, WGROUP).transpose(0, 2, 1, 3)


def kernel(x, pre_norm_g, w_in, w_ret_out, conv_w, conv_b, conv_ln_g, conv_ln_b, w_conv_out, w_o, post_norm_g):
    depth = w_in.shape[0]
    batch, seq, d_model = x.shape
    assert d_model == D_MODEL and seq % ROWS == 0 and ROWS % CHUNK == 0 and ROWS % CONV_ROWS == 0
    assert w_in.shape[1:] == (D_MODEL, IN_WIDTH) and conv_w.shape[1:] == (CONV_K, D_MODEL)
    tables = _tables(seq)
    row = lambda p: p.reshape(depth, 1, D_MODEL)
    params = (row(pre_norm_g), _column_groups(w_in), _column_groups(w_ret_out),
              jnp.broadcast_to(conv_w[:, :, None, :], (depth, CONV_K, SUBLANES, D_MODEL)),
              row(conv_b), row(conv_ln_g), row(conv_ln_b),
              _column_groups(w_conv_out), _column_groups(w_o), row(post_norm_g))
    h = x
    for layer in range(depth):
        h = _layer_call(layer, h, tables, params)
    return h
```
